```python
import math
import jax, jax.numpy as jnp
from jax import lax
import numpy as np

D_MODEL = 2048
BATCH = 2
SEQ = 16384
DEPTH = 4

BW = D_MODEL // 4
N_BRANCH = 3
HG_HEADS = 4
HG_DIM = BW // HG_HEADS
HG_CHUNK = 16
S5_GROUP = 16
S5_GROUPS = BW // S5_GROUP
S5_STATE = 64
DT_MIN = 0.001
DT_MAX = 0.1
SB_HEADS = 4
SB_DIM = BW // SB_HEADS
SB_BLOCK = 128
D_FF = 256 * ((8 * D_MODEL // 3 + 255) // 256)
CONV_W = 3
EPS = 1e-6
IN_COLS = 8 * BW + N_BRANCH * D_MODEL

kernel_name = "hybrid_hgrn2_s5_stickbreak_block"


def rms_norm(x, g):
    xf = x.astype(jnp.float32)
    y = xf * lax.rsqrt(jnp.mean(xf * xf, axis=-1, keepdims=True) + EPS)
    return (y * g.astype(jnp.float32)).astype(x.dtype)


def hgrn2_mixer(q, f_logit, i_in, g, lb, norm_g):
    f32 = jnp.float32
    B_, L_, _ = q.shape
    nc = L_ // HG_CHUNK
    lb = lb.astype(f32)
    logf = jnp.logaddexp(jnp.log(lb), jnp.log1p(-lb) + jax.nn.log_sigmoid(f_logit.astype(f32)))
    k = -jnp.expm1(logf)

    def to_chunks(t):
        return t.astype(f32).reshape(B_, nc, HG_CHUNK, HG_HEADS, HG_DIM).transpose(1, 0, 3, 2, 4)

    qc, kc, vc, lfc = to_chunks(q), to_chunks(k), to_chunks(i_in), to_chunks(logf)
    causal = jnp.tril(jnp.ones((HG_CHUNK, HG_CHUNK), dtype=bool))

    def step(S, inp):
        qb, kb, vb, lfb = inp
        b = jnp.cumsum(lfb, axis=2)
        diff = b[:, :, :, None, :] - b[:, :, None, :, :]
        decay = jnp.exp(jnp.where(causal[:, :, None], diff, -jnp.inf))
        scores = jnp.einsum('bhtd,bhsd,bhtsd->bhts', qb, kb, decay)
        o = scores @ vb + jnp.einsum('bhtd,bhde->bhte', qb * jnp.exp(b), S)
        b_last = b[:, :, -1:, :]
        S_new = jnp.exp(b_last[:, :, 0, :, None]) * S + jnp.einsum(
            'bhsd,bhse->bhde', kb * jnp.exp(b_last - b), vb)
        return S_new, o

    S0 = jnp.zeros((B_, HG_HEADS, HG_DIM, HG_DIM), f32)
    _, o = lax.scan(step, S0, (qc, kc, vc, lfc))
    o = o.transpose(1, 0, 3, 2, 4).reshape(B_, L_, HG_HEADS, HG_DIM)
    o = o * lax.rsqrt(jnp.mean(o * o, axis=-1, keepdims=True) + EPS)
    o = o.reshape(B_, L_, BW) * norm_g.astype(f32) * jax.nn.silu(g.astype(f32))
    return o.astype(q.dtype)


def s5_mixer(u, a_re, a_im, b_re, b_im, c_re, c_im, d_skip, log_dt, w_glu):
    f32 = jnp.float32
    B_, L_, _ = u.shape
    uf = u.astype(f32)
    ug = uf.reshape(B_, L_, S5_GROUPS, S5_GROUP)
    dt = jnp.exp(log_dt.astype(f32))[:, None]
    lr, li = a_re.astype(f32), a_im.astype(f32)
    mag = jnp.exp(lr * dt)
    ab_re, ab_im = mag * jnp.cos(li * dt), mag * jnp.sin(li * dt)
    den = lr * lr + li * li
    nr, ni = ab_re - 1.0, ab_im
    z_re = (nr * lr + ni * li) / den
    z_im = (ni * lr - nr * li) / den
    br, bi = b_re.astype(f32), b_im.astype(f32)
    bb_re = z_re[..., None] * br - z_im[..., None] * bi
    bb_im = z_re[..., None] * bi + z_im[..., None] * br
    bu_re = jnp.einsum('gnc,blgc->blgn', bb_re, ug)
    bu_im = jnp.einsum('gnc,blgc->blgn', bb_im, ug)
    a_sc_re = jnp.broadcast_to(ab_re, bu_re.shape)
    a_sc_im = jnp.broadcast_to(ab_im, bu_im.shape)

    def combine(e1, e2):
        a1r, a1i, b1r, b1i = e1
        a2r, a2i, b2r, b2i = e2
        return (a2r * a1r - a2i * a1i, a2r * a1i + a2i * a1r,
                a2r * b1r - a2i * b1i + b2r, a2r * b1i + a2i * b1r + b2i)

    _, _, xs_re, xs_im = lax.associative_scan(combine, (a_sc_re, a_sc_im, bu_re, bu_im), axis=1)
    y = (jnp.einsum('gcn,blgn->blgc', c_re.astype(f32), xs_re)
         - jnp.einsum('gcn,blgn->blgc', c_im.astype(f32), xs_im))
    y = y.reshape(B_, L_, BW) + d_skip.astype(f32) * uf
    a = jax.nn.gelu(y)
    hg = a @ w_glu.astype(f32)
    out = hg[..., :BW] * jax.nn.sigmoid(hg[..., BW:])
    return out.astype(u.dtype)


def stick_breaking_mixer(q, k, v, q_g, k_g):
    f32 = jnp.float32
    B_, L_, _ = q.shape

    def heads(t):
        return t.astype(f32).reshape(B_, L_, SB_HEADS, SB_DIM).transpose(0, 2, 1, 3)

    qh = rms_norm(heads(q), q_g)
    kh = rms_norm(heads(k), k_g)
    vh = heads(v)
    nb = L_ // SB_BLOCK
    scale = SB_DIM ** -0.5
    outs = []
    for i in range(nb):
        n = i + 1
        qb = qh[:, :, i * SB_BLOCK:(i + 1) * SB_BLOCK]
        kb = kh[:, :, :n * SB_BLOCK].reshape(B_, SB_HEADS, n, SB_BLOCK, SB_DIM)
        vb = vh[:, :, :n * SB_BLOCK].reshape(B_, SB_HEADS, n, SB_BLOCK, SB_DIM)
        q_pos = i * SB_BLOCK + jnp.arange(SB_BLOCK)
        k_pos = jnp.arange(n * SB_BLOCK).reshape(n, SB_BLOCK)
        mask = k_pos[None] < q_pos[:, None, None]
        z = jnp.einsum('bhqd,bhnkd->bhqnk', qb, kb) * scale
        log_keep = jnp.where(mask, jax.nn.log_sigmoid(-z), 0.0)
        within = lax.cumsum(log_keep, axis=4, reverse=True) - log_keep
        blk_tot = jnp.sum(log_keep, axis=4)
        later = lax.cumsum(blk_tot, axis=3, reverse=True) - blk_tot
        w = jnp.where(mask, jnp.exp(jax.nn.log_sigmoid(z) + within + later[..., None]), 0.0)
        outs.append(jnp.einsum('bhqnk,bhnkd->bhqd', w, vb))
    o = jnp.concatenate(outs, axis=2)
    o = o.transpose(0, 2, 1, 3).reshape(B_, L_, BW)
    return o.astype(q.dtype)


def causal_depthwise_conv(u, w):
    return lax.conv_general_dilated(
        u, w[:, None, :].astype(u.dtype), window_strides=(1,), padding=[(CONV_W - 1, 0)],
        dimension_numbers=('NWC', 'WIO', 'NWC'), feature_group_count=u.shape[-1])


def setup_inputs(seed: int = 0) -> dict:
    key = jax.random.key(seed)
    ks = jax.random.split(key, 24)
    f32 = jnp.float32
    nrm = lambda k, shape, s: jax.random.normal(k, shape, f32) * s
    gain = lambda k, shape: 1.0 + 0.02 * jax.random.normal(k, shape, f32)
    return {
        "x": nrm(ks[0], (BATCH, SEQ, D_MODEL), 1.0),
        "w_in": nrm(ks[1], (DEPTH, D_MODEL, IN_COLS), D_MODEL ** -0.5),
        "norm_mix_g": gain(ks[2], (DEPTH, D_MODEL)),
        "hg_lb_logits": nrm(ks[3], (DEPTH, BW), 0.1),
        "hg_norm_g": gain(ks[4], (DEPTH, BW)),
        "s5_a_re": -0.5 + nrm(ks[5], (DEPTH, S5_GROUPS, S5_STATE), 0.01),
        "s5_a_im": math.pi * jnp.arange(S5_STATE, dtype=f32) + nrm(ks[6], (DEPTH, S5_GROUPS, S5_STATE), 0.01),
        "s5_b_re": nrm(ks[7], (DEPTH, S5_GROUPS, S5_STATE, S5_GROUP), (2 * S5_GROUP) ** -0.5),
        "s5_b_im": nrm(ks[8], (DEPTH, S5_GROUPS, S5_STATE, S5_GROUP), (2 * S5_GROUP) ** -0.5),
        "s5_c_re": nrm(ks[9], (DEPTH, S5_GROUPS, S5_GROUP, S5_STATE), (2 * S5_STATE) ** -0.5),
        "s5_c_im": nrm(ks[10], (DEPTH, S5_GROUPS, S5_GROUP, S5_STATE), (2 * S5_STATE) ** -0.5),
        "s5_d": nrm(ks[11], (DEPTH, BW), 1.0),
        "s5_log_dt": jax.random.uniform(ks[12], (DEPTH, S5_GROUPS), f32, math.log(DT_MIN), math.log(DT_MAX)),
        "s5_w_glu": nrm(ks[13], (DEPTH, BW, 2 * BW), BW ** -0.5),
        "sb_q_g": gain(ks[14], (DEPTH, SB_DIM)),
        "sb_k_g": gain(ks[15], (DEPTH, SB_DIM)),
        "w_branch": nrm(ks[16], (DEPTH, N_BRANCH, BW, D_MODEL), BW ** -0.5),
        "w_out": nrm(ks[17], (DEPTH, D_MODEL, D_MODEL), D_MODEL ** -0.5),
        "norm_ffn_g": gain(ks[18], (DEPTH, D_MODEL)),
        "w_up": nrm(ks[19], (DEPTH, D_MODEL, 2 * D_FF), D_MODEL ** -0.5),
        "conv_w": nrm(ks[20], (DEPTH, CONV_W, 2 * D_FF), CONV_W ** -0.5),
        "w_down": nrm(ks[21], (DEPTH, D_FF, D_MODEL), D_FF ** -0.5),
    }


def reference(x, w_in, norm_mix_g, hg_lb_logits, hg_norm_g, s5_a_re, s5_a_im, s5_b_re, s5_b_im,
              s5_c_re, s5_c_im, s5_d, s5_log_dt, s5_w_glu, sb_q_g, sb_k_g, w_branch, w_out,
              norm_ffn_g, w_up, conv_w, w_down):
    lb_all = jnp.cumsum(jax.nn.softmax(hg_lb_logits.astype(jnp.float32), axis=0), axis=0)
    lb_all = lb_all - lb_all[0:1]
    split_at = [BW * i for i in range(1, 9)]
    for l in range(DEPTH):
        h = rms_norm(x, norm_mix_g[l])
        proj = h @ w_in[l]
        hq, hf, hi, hg, su, sq, sk, sv, gates = jnp.split(proj, split_at, axis=-1)
        o_a = hgrn2_mixer(hq, hf, hi, hg, lb_all[l], hg_norm_g[l])
        o_b = s5_mixer(su, s5_a_re[l], s5_a_im[l], s5_b_re[l], s5_b_im[l], s5_c_re[l], s5_c_im[l],
                       s5_d[l], s5_log_dt[l], s5_w_glu[l])
        o_c = stick_breaking_mixer(sq, sk, sv, sb_q_g[l], sb_k_g[l])
        gate = jax.nn.sigmoid(gates)
        merged = (gate[..., :D_MODEL] * (o_a @ w_branch[l, 0])
                  + gate[..., D_MODEL:2 * D_MODEL] * (o_b @ w_branch[l, 1])
                  + gate[..., 2 * D_MODEL:] * (o_c @ w_branch[l, 2]))
        x = x + merged @ w_out[l]
        hf2 = rms_norm(x, norm_ffn_g[l])
        up = causal_depthwise_conv(hf2 @ w_up[l], conv_w[l])
        x = x + (jax.nn.silu(up[..., :D_FF]) * up[..., D_FF:]) @ w_down[l]
    return x
```

```python
import functools
import math

import numpy as np
import jax
import jax.numpy as jnp
from jax import lax
from jax.experimental import pallas as pl
from jax.experimental.pallas import tpu as pltpu

F32 = jnp.float32
BF16 = jnp.bfloat16
EPS = 1e-6

N_HEADS = 4
HEAD_DIM = 128
BW = N_HEADS * HEAD_DIM
S5_GROUP = 16
S5_STATE = 64
S5_GROUPS = BW // S5_GROUP
CONV_W = 3

LANES = 128
SUBLANES = 8
VMEM_LIMIT_BYTES = 56 * 1024 * 1024

HG_BLOCK = 128
HG_LEVELS = (64, 32, 16, 8, 4, 2, 1)
S5_TILE = 8
S5_QGROUPS = LANES // S5_GROUP
S5_NQ = BW // LANES
S5_QSTATE = S5_QGROUPS * S5_STATE
SB_TK = 256


def _dot(a, b):
    return jnp.dot(a, b, preferred_element_type=F32)


def _dot_nt(a, b):
    return lax.dot_general(a, b, (((1,), (1,)), ((), ())), preferred_element_type=F32)


def _dot_tn(a, b):
    return lax.dot_general(a, b, (((0,), (0,)), ((), ())), preferred_element_type=F32)


def _split2(x):
    hi = x.astype(BF16)
    lo = (x - hi.astype(F32)).astype(BF16)
    return hi, lo


def _sigmoid(x):
    return 1.0 / (1.0 + jnp.exp(-x))


def _silu(x):
    return x * _sigmoid(x)


def _gelu_tanh(x):
    c = math.sqrt(2.0 / math.pi)
    return 0.5 * x * (1.0 + jnp.tanh(c * (x + 0.044715 * (x * x * x))))


def _rms(x, g):
    ms = jnp.mean(x * x, axis=-1, keepdims=True)
    return x * lax.rsqrt(ms + EPS) * g


def _cparams(sem):
    return pltpu.CompilerParams(dimension_semantics=sem, vmem_limit_bytes=VMEM_LIMIT_BYTES)


def _inproj_kernel(x_ref, g_ref, w_ref, o_ref, h_ref):
    @pl.when(pl.program_id(1) == 0)
    def _():
        h_ref[...] = _rms(x_ref[...], g_ref[...]).astype(BF16)

    o_ref[...] = _dot(h_ref[...], w_ref[...]).astype(o_ref.dtype)


def _inproj(x2, g, w, tm, tn):
    t, d = x2.shape
    n = w.shape[1]
    return pl.pallas_call(
        _inproj_kernel,
        grid=(t // tm, n // tn),
        in_specs=[
            pl.BlockSpec((tm, d), lambda i, j: (i, 0)),
            pl.BlockSpec((1, d), lambda i, j: (0, 0)),
            pl.BlockSpec((d, tn), lambda i, j: (0, j)),
        ],
        out_specs=pl.BlockSpec((tm, tn), lambda i, j: (i, j)),
        out_shape=jax.ShapeDtypeStruct((t, n), BF16),
        scratch_shapes=[pltpu.VMEM((tm, d), BF16)],
        compiler_params=_cparams(("arbitrary", "arbitrary")),
        name="inproj",
    )(x2, g.reshape(1, d), w)


def _hgrn2_decay_matrix():
    c = HG_BLOCK
    r = np.arange(c)[:, None]
    j = np.arange(c)[None, :]
    blocks = [(j <= r), (j > r)]
    for lvl in HG_LEVELS:
        ref = (r // (2 * lvl)) * (2 * lvl) + lvl - 1
        upper = (r % (2 * lvl)) >= lvl
        blocks.append(np.where(upper, (j > ref) & (j <= r), (j > r) & (j <= ref)))
    return np.concatenate(blocks, axis=0).astype(np.float32)


def _hgrn2_kernel(q_ref, f_ref, i_ref, g_ref, p_ref, a_ref, o_ref, st_ref):
    c = HG_BLOCK
    nb = q_ref.shape[0]

    @pl.when(pl.program_id(0) == 0)
    def _():
        st_ref[...] = jnp.zeros_like(st_ref)

    la = p_ref[0:1, :]
    l1 = p_ref[1:2, :]
    one_m_lb = p_ref[2:3, :]
    norm_g = p_ref[3:4, :]
    amat = a_ref[...]

    row = lax.broadcasted_iota(jnp.int32, (c, c), 0)
    col = lax.broadcasted_iota(jnp.int32, (c, c), 1)

    for b in range(nb):
        z = f_ref[b].astype(F32)
        sp = jnp.log(1.0 + jnp.exp(-jnp.abs(z)))
        y = l1 + (jnp.minimum(z, 0.0) - sp)
        logf = jnp.maximum(la, y) + jnp.log(1.0 + jnp.exp(-jnp.abs(la - y)))
        kk = one_m_lb * jnp.exp(jnp.minimum(-z, 0.0) - sp)
        lf_hi, lf_lo = _split2(logf)
        dec = jnp.exp(_dot(amat, lf_hi) + _dot(amat, lf_lo))
        q = q_ref[b].astype(F32)
        v = i_ref[b]
        gate = g_ref[b].astype(F32)
        for h in range(N_HEADS):
            sl = slice(h * HEAD_DIM, (h + 1) * HEAD_DIM)
            qh, kh, vh = q[:, sl], kk[:, sl], v[:, sl]
            scores = jnp.where(row == col, _dot_nt(qh.astype(BF16), kh.astype(BF16)), 0.0)
            for li, lvl in enumerate(HG_LEVELS):
                e = dec[(2 + li) * c:(3 + li) * c, sl]
                sh = int(math.log2(lvl))
                upper = ((row >> sh) & 1) == 1
                xq = jnp.where(upper, e * qh, 0.0).astype(BF16)
                xk = jnp.where(upper, 0.0, e * kh).astype(BF16)
                same = (row >> (sh + 1)) == (col >> (sh + 1))
                scores = scores + jnp.where(same, _dot_nt(xq, xk), 0.0)
            st = st_ref[b, h]
            qdec = (qh * dec[0:c, sl]).astype(BF16)
            o = _dot(scores.astype(BF16), vh) + _dot_nt(qdec, st.astype(BF16))
            kdec = (kh * dec[c:2 * c, sl]).astype(BF16)
            st_ref[b, h] = st * dec[c - 1:c, sl] + _dot_tn(vh, kdec)
            o = o * lax.rsqrt(jnp.mean(o * o, axis=-1, keepdims=True) + EPS)
            o_ref[b, :, sl] = (o * norm_g[:, sl] * _silu(gate[:, sl])).astype(o_ref.dtype)


def _hgrn2(proj3, params, amat):
    nb, l, _ = proj3.shape
    c = HG_BLOCK

    def col(k):
        return pl.BlockSpec((nb, c, BW), lambda i, k=k: (0, i, k))

    return pl.pallas_call(
        _hgrn2_kernel,
        grid=(l // c,),
        in_specs=[col(0), col(1), col(2), col(3),
                  pl.BlockSpec(params.shape, lambda i: (0, 0)),
                  pl.BlockSpec(amat.shape, lambda i: (0, 0))],
        out_specs=pl.BlockSpec((nb, c, BW), lambda i: (0, i, 0)),
        out_shape=jax.ShapeDtypeStruct((nb, l, BW), BF16),
        scratch_shapes=[pltpu.VMEM((nb, N_HEADS, HEAD_DIM, HEAD_DIM), F32)],
        compiler_params=_cparams(("arbitrary",)),
        name="hgrn2",
    )(proj3, proj3, proj3, proj3, params, amat)


def _s5_tables(a_re, a_im, b_re, b_im, c_re, c_im, d_skip, log_dt):
    r = S5_TILE
    dt = jnp.exp(log_dt)[:, None]
    lr, li = a_re, a_im
    mag = jnp.exp(lr * dt)
    ab_re, ab_im = mag * jnp.cos(li * dt), mag * jnp.sin(li * dt)
    den = lr * lr + li * li
    nr, ni = ab_re - 1.0, ab_im
    z_re = (nr * lr + ni * li) / den
    z_im = (ni * lr - nr * li) / den
    bb_re = z_re[..., None] * b_re - z_im[..., None] * b_im
    bb_im = z_re[..., None] * b_im + z_im[..., None] * b_re
    def powers(k):
        k = k[:, None, None]
        pmag = jnp.exp(k * (lr * dt))
        return pmag * jnp.cos(k * (li * dt)), pmag * jnp.sin(k * (li * dt))

    p_re, p_im = powers(jnp.arange(r + 1, dtype=F32))

    nq, qg, n, gc = S5_NQ, S5_QGROUPS, S5_STATE, S5_GROUP
    eye = jnp.eye(qg, dtype=F32)

    pw_re, pw_im = powers(r - 1.0 - jnp.arange(r, dtype=F32))
    wb_re = pw_re[..., None] * bb_re[None] - pw_im[..., None] * bb_im[None]
    wb_im = pw_re[..., None] * bb_im[None] + pw_im[..., None] * bb_re[None]

    def expand_b(w):
        w = w.reshape(r, nq, qg, n, gc)
        w = jnp.einsum('jqgnc,gh->qjgchn', w, eye)
        return w.reshape(nq, r * qg * gc, qg * n)

    wb = jnp.concatenate([expand_b(wb_re), expand_b(wb_im)], axis=-1)

    q_re, q_im = p_re[1:], p_im[1:]
    cc_re = c_re[None] * q_re[:, :, None, :] - c_im[None] * q_im[:, :, None, :]
    cc_im = c_re[None] * q_im[:, :, None, :] + c_im[None] * q_re[:, :, None, :]

    def expand_c(w):
        w = w.reshape(r, nq, qg, gc, n)
        w = jnp.einsum('iqgcn,gh->qgnihc', w, eye)
        return w.reshape(nq, qg * n, r * qg * gc)

    wc = jnp.concatenate([expand_c(cc_re), -expand_c(cc_im)], axis=1)

    cb_re = (jnp.einsum('gcn,lgn,gnd->lgcd', c_re, p_re[:r], bb_re) - jnp.einsum('gcn,lgn,gnd->lgcd', c_re, p_im[:r], bb_im)
             - jnp.einsum('gcn,lgn,gnd->lgcd', c_im, p_re[:r], bb_im) - jnp.einsum('gcn,lgn,gnd->lgcd', c_im, p_im[:r], bb_re))
    lag = jnp.arange(r)[None, :] - jnp.arange(r)[:, None]
    kt = jnp.where((lag >= 0)[:, :, None, None, None], cb_re[jnp.clip(lag, 0, r - 1)], 0.0)
    kt = kt.reshape(r, r, nq, qg, gc, gc)
    wk = jnp.einsum('jiqgcd,gh->qjgdihc', kt, eye).reshape(nq, r * qg * gc, r * qg * gc)

    a8 = jnp.concatenate([p_re[r].reshape(1, -1), p_im[r].reshape(1, -1)], axis=0)
    d8 = jnp.tile(d_skip.reshape(1, BW), (1, r))
    return wb.astype(BF16), wk.astype(BF16), wc.astype(BF16), a8, d8


def _s5_kernel(u_ref, wb_ref, wk_ref, wc_ref, a8_ref, d8_ref, o_ref, z_ref, xp_ref, x_ref):
    rows = u_ref.shape[1]
    ns = S5_GROUPS * S5_STATE
    qs = S5_QSTATE

    @pl.when(pl.program_id(1) == 0)
    def _():
        x_ref[...] = jnp.zeros_like(x_ref)

    def gather_q(ref, q):
        return jnp.concatenate(
            [ref[0, :, j * BW + q * LANES:j * BW + (q + 1) * LANES] for j in range(S5_TILE)], axis=-1)

    for q in range(S5_NQ):
        zq = _dot(gather_q(u_ref, q), wb_ref[q])
        z_ref[:, q * qs:(q + 1) * qs] = zq[:, :qs]
        z_ref[:, ns + q * qs:ns + (q + 1) * qs] = zq[:, qs:]

    ar, ai = a8_ref[0:1, :], a8_ref[1:2, :]

    def step(r, carry):
        xr, xi = carry
        xp_ref[pl.ds(r, 1), 0:ns] = xr
        xp_ref[pl.ds(r, 1), ns:2 * ns] = xi
        zr = z_ref[pl.ds(r, 1), 0:ns]
        zi = z_ref[pl.ds(r, 1), ns:2 * ns]
        return ar * xr - ai * xi + zr, ar * xi + ai * xr + zi

    xr, xi = lax.fori_loop(0, rows, step, (x_ref[0:1, 0:ns], x_ref[0:1, ns:2 * ns]))
    x_ref[0:1, 0:ns] = xr
    x_ref[0:1, ns:2 * ns] = xi

    for q in range(S5_NQ):
        xq = jnp.concatenate([xp_ref[:, q * qs:(q + 1) * qs],
                              xp_ref[:, ns + q * qs:ns + (q + 1) * qs]], axis=-1).astype(BF16)
        yq = _dot(gather_q(u_ref, q), wk_ref[q]) + _dot(xq, wc_ref[q])
        for i in range(S5_TILE):
            sl = slice(i * BW + q * LANES, i * BW + (q + 1) * LANES)
            y = yq[:, i * LANES:(i + 1) * LANES] + d8_ref[:, sl] * u_ref[0, :, sl].astype(F32)
            o_ref[0, :, sl] = _gelu_tanh(y).astype(o_ref.dtype)


def _s5(u8, tables, rows):
    nb, lr, w = u8.shape
    wb, wk, wc, a8, d8 = tables
    full3 = lambda b, i: (0, 0, 0)
    full2 = lambda b, i: (0, 0)
    return pl.pallas_call(
        _s5_kernel,
        grid=(nb, lr // rows),
        in_specs=[pl.BlockSpec((1, rows, w), lambda b, i: (b, i, 0)),
                  pl.BlockSpec(wb.shape, full3), pl.BlockSpec(wk.shape, full3), pl.BlockSpec(wc.shape, full3),
                  pl.BlockSpec(a8.shape, full2), pl.BlockSpec(d8.shape, full2)],
        out_specs=pl.BlockSpec((1, rows, w), lambda b, i: (b, i, 0)),
        out_shape=jax.ShapeDtypeStruct((nb, lr, w), BF16),
        scratch_shapes=[pltpu.VMEM((rows, w), F32), pltpu.VMEM((rows, w), F32), pltpu.VMEM((SUBLANES, w), F32)],
        compiler_params=_cparams(("arbitrary", "arbitrary")),
        name="s5",
    )(u8, wb, wk, wc, a8, d8)


def _stickbreak_kernel(q_ref, k_ref, v_ref, qg_ref, kg_ref, u_ref, o_ref, kh_ref, acc_ref, later_ref):
    tq = q_ref.shape[1]
    l = k_ref.shape[1]
    tk = SB_TK
    qi = pl.program_id(2)
    kchunk = min(l, 1024)

    @pl.when(qi == 0)
    def _():
        def body(c, _):
            r0 = pl.multiple_of(c * kchunk, kchunk)
            kh_ref[pl.ds(r0, kchunk), :] = _rms(k_ref[0, pl.ds(r0, kchunk), :].astype(F32), kg_ref[...]).astype(BF16)
            return 0
        lax.fori_loop(0, l // kchunk, body, 0)

    qh = (_rms(q_ref[0].astype(F32), qg_ref[...]) * (HEAD_DIM ** -0.5)).astype(BF16)
    umat = u_ref[...]
    acc_ref[...] = jnp.zeros_like(acc_ref)
    later_ref[...] = jnp.zeros_like(later_ref)

    def block(k0, masked):
        kb = kh_ref[pl.ds(k0, tk), :]
        vb = v_ref[0, pl.ds(k0, tk), :]
        z = _dot_nt(qh, kb)
        sp = jnp.log(1.0 + jnp.exp(-jnp.abs(z)))
        lk = -(jnp.maximum(z, 0.0) + sp)
        ls = jnp.minimum(z, 0.0) - sp
        if masked:
            qpos = qi * tq + lax.broadcasted_iota(jnp.int32, (tq, tk), 0)
            kpos = k0 + lax.broadcasted_iota(jnp.int32, (tq, tk), 1)
            keep = kpos < qpos
            lk = jnp.where(keep, lk, 0.0)
        hi, lo = _split2(lk)
        within = _dot(hi, umat) + _dot(lo, umat)
        w = jnp.exp(ls + within + later_ref[...])
        if masked:
            w = jnp.where(keep, w, 0.0)
        acc_ref[...] += _dot(w.astype(BF16), vb)
        later_ref[...] += jnp.sum(lk, axis=-1, keepdims=True)

    for d in reversed(range(tq // tk)):
        block(pl.multiple_of(qi * tq + d * tk, tk), True)

    def body(n, _):
        kbi = qi * (tq // tk) - 1 - n
        block(pl.multiple_of(kbi * tk, tk), False)
        return 0

    lax.fori_loop(0, qi * (tq // tk), body, 0)
    o_ref[0] = acc_ref[...].astype(o_ref.dtype)


def _stickbreak(proj3, q_g, k_g, tq):
    nb, l, _ = proj3.shape
    tk = SB_TK
    umat = jnp.asarray(np.tril(np.ones((tk, tk), np.float32), -1), BF16)
    qcol, kcol, vcol = 5 * N_HEADS, 6 * N_HEADS, 7 * N_HEADS
    return pl.pallas_call(
        _stickbreak_kernel,
        grid=(nb, N_HEADS, l // tq),
        in_specs=[pl.BlockSpec((1, tq, HEAD_DIM), lambda b, h, i: (b, i, qcol + h)),
                  pl.BlockSpec((1, l, HEAD_DIM), lambda b, h, i: (b, 0, kcol + h)),
                  pl.BlockSpec((1, l, HEAD_DIM), lambda b, h, i: (b, 0, vcol + h)),
                  pl.BlockSpec((1, HEAD_DIM), lambda b, h, i: (0, 0)),
                  pl.BlockSpec((1, HEAD_DIM), lambda b, h, i: (0, 0)),
                  pl.BlockSpec((tk, tk), lambda b, h, i: (0, 0))],
        out_specs=pl.BlockSpec((1, tq, HEAD_DIM), lambda b, h, i: (b, i, h)),
        out_shape=jax.ShapeDtypeStruct((nb, l, BW), BF16),
        scratch_shapes=[pltpu.VMEM((l, HEAD_DIM), BF16), pltpu.VMEM((tq, HEAD_DIM), F32), pltpu.VMEM((tq, 1), F32)],
        compiler_params=_cparams(("arbitrary", "arbitrary", "arbitrary")),
        name="stickbreak",
    )(proj3, proj3, proj3, q_g.reshape(1, HEAD_DIM), k_g.reshape(1, HEAD_DIM), umat)


def _merge_kernel(x_ref, oa_ref, sa_ref, oc_ref, ga_ref, gb_ref, gc_ref, wglu_ref, wbr_ref, wout_ref, o_ref):
    hg = _dot(sa_ref[...], wglu_ref[...])
    ob = (hg[:, :BW] * _sigmoid(hg[:, BW:])).astype(BF16)
    m = (_sigmoid(ga_ref[...].astype(F32)) * _dot(oa_ref[...], wbr_ref[0])
         + _sigmoid(gb_ref[...].astype(F32)) * _dot(ob, wbr_ref[1])
         + _sigmoid(gc_ref[...].astype(F32)) * _dot(oc_ref[...], wbr_ref[2]))
    o_ref[...] = x_ref[...] + _dot(m.astype(BF16), wout_ref[...])


def _merge(x2, oa, sa, oc, proj, wglu, wbr, wout, tm):
    t, d = x2.shape
    gate0 = (8 * BW) // d
    row = lambda i: (i, 0)
    const = pl.Buffered(1)
    return pl.pallas_call(
        _merge_kernel,
        grid=(t // tm,),
        in_specs=[pl.BlockSpec((tm, d), row),
                  pl.BlockSpec((tm, BW), row), pl.BlockSpec((tm, BW), row), pl.BlockSpec((tm, BW), row),
                  pl.BlockSpec((tm, d), lambda i: (i, gate0)),
                  pl.BlockSpec((tm, d), lambda i: (i, gate0 + 1)),
                  pl.BlockSpec((tm, d), lambda i: (i, gate0 + 2)),
                  pl.BlockSpec(wglu.shape, lambda i: (0, 0), pipeline_mode=const),
                  pl.BlockSpec(wbr.shape, lambda i: (0, 0, 0), pipeline_mode=const),
                  pl.BlockSpec(wout.shape, lambda i: (0, 0), pipeline_mode=const)],
        out_specs=pl.BlockSpec((tm, d), row),
        out_shape=jax.ShapeDtypeStruct((t, d), F32),
        compiler_params=_cparams(("arbitrary",)),
        name="merge",
    )(x2, oa, sa, oc, proj, proj, proj, wglu, wbr, wout)


def _ffn_kernel(xh_ref, x_ref, g_ref, wa_ref, wb_ref, cwa_ref, cwb_ref, wd_ref, o_ref, hn_ref, *, tiles_per_seq):
    i = pl.program_id(0)
    halo = SUBLANES

    @pl.when(pl.program_id(1) == 0)
    def _():
        first = (i % tiles_per_seq) == 0
        hh = _rms(xh_ref[...], g_ref[...])
        hn_ref[0:halo, :] = jnp.where(first, 0.0, hh).astype(BF16)
        hn_ref[halo:, :] = _rms(x_ref[...], g_ref[...]).astype(BF16)
        o_ref[...] = x_ref[...]

    hn = hn_ref[...]

    def conv(u, cw):
        return cw[2:3, :] * u[halo:, :] + cw[1:2, :] * u[halo - 1:-1, :] + cw[0:1, :] * u[halo - 2:-2, :]

    ca = conv(_dot(hn, wa_ref[...]), cwa_ref[...])
    cb = conv(_dot(hn, wb_ref[...]), cwb_ref[...])
    act = (_silu(ca) * cb).astype(BF16)
    o_ref[...] += _dot(act, wd_ref[...])


def _ffn(x2, g, w_up, conv_w, w_down, tm, tf, seq_len):
    t, d = x2.shape
    dff = w_down.shape[0]
    nf = dff // tf
    hb = tm // SUBLANES
    return pl.pallas_call(
        functools.partial(_ffn_kernel, tiles_per_seq=seq_len // tm),
        grid=(t // tm, nf),
        in_specs=[pl.BlockSpec((SUBLANES, d), lambda i, j: (jnp.maximum(i * hb - 1, 0), 0)),
                  pl.BlockSpec((tm, d), lambda i, j: (i, 0)),
                  pl.BlockSpec((1, d), lambda i, j: (0, 0)),
                  pl.BlockSpec((d, tf), lambda i, j: (0, j)),
                  pl.BlockSpec((d, tf), lambda i, j: (0, j + nf)),
                  pl.BlockSpec((CONV_W, tf), lambda i, j: (0, j)),
                  pl.BlockSpec((CONV_W, tf), lambda i, j: (0, j + nf)),
                  pl.BlockSpec((tf, d), lambda i, j: (j, 0))],
        out_specs=pl.BlockSpec((tm, d), lambda i, j: (i, 0)),
        out_shape=jax.ShapeDtypeStruct((t, d), F32),
        scratch_shapes=[pltpu.VMEM((tm + SUBLANES, d), BF16)],
        compiler_params=_cparams(("arbitrary", "arbitrary")),
        name="ffn",
    )(x2, x2, g.reshape(1, d), w_up, w_up, conv_w, conv_w, w_down)


def _tiles(seq_len):
    return dict(
        inproj_tm=min(1024, seq_len), inproj_tn=1024,
        s5_rows=min(256, seq_len // S5_TILE),
        sb_tq=min(512, seq_len),
        merge_tm=min(256, seq_len),
        ffn_tm=min(512, seq_len), ffn_tf=512,
    )


def kernel(x, w_in, norm_mix_g, hg_lb_logits, hg_norm_g, s5_a_re, s5_a_im, s5_b_re, s5_b_im, s5_c_re, s5_c_im, s5_d, s5_log_dt, s5_w_glu, sb_q_g, sb_k_g, w_branch, w_out, norm_ffn_g, w_up, conv_w, w_down):
    nb, l, d = x.shape
    depth = w_in.shape[0]
    t = nb * l
    tl = _tiles(l)
    assert l % HG_BLOCK == 0 and l % SB_TK == 0 and w_in.shape[2] == 8 * BW + 3 * d

    lb = jnp.cumsum(jax.nn.softmax(hg_lb_logits.astype(F32), axis=0), axis=0)
    lb = lb - lb[0:1]
    zeros = jnp.zeros_like(lb)
    hg_params = jnp.stack([jnp.log(lb), jnp.log1p(-lb), 1.0 - lb, hg_norm_g.astype(F32),
                           zeros, zeros, zeros, zeros], axis=1)
    amat = jnp.asarray(_hgrn2_decay_matrix(), BF16)

    w_in_b, w_glu_b, w_br_b = w_in.astype(BF16), s5_w_glu.astype(BF16), w_branch.astype(BF16)
    w_out_b, w_up_b, w_down_b = w_out.astype(BF16), w_up.astype(BF16), w_down.astype(BF16)

    x2 = x.reshape(t, d)
    for lyr in range(depth):
        proj = _inproj(x2, norm_mix_g[lyr], w_in_b[lyr], tl["inproj_tm"], tl["inproj_tn"])
        proj3 = proj.reshape(nb, l, -1)
        o_a = _hgrn2(proj3, hg_params[lyr], amat)
        tables = _s5_tables(s5_a_re[lyr], s5_a_im[lyr], s5_b_re[lyr], s5_b_im[lyr], s5_c_re[lyr], s5_c_im[lyr],
                            s5_d[lyr], s5_log_dt[lyr])
        u8 = proj3[:, :, 4 * BW:5 * BW].reshape(nb, l // S5_TILE, S5_TILE * BW)
        s_a = _s5(u8, tables, tl["s5_rows"])
        o_c = _stickbreak(proj3, sb_q_g[lyr], sb_k_g[lyr], tl["sb_tq"])
        x2 = _merge(x2, o_a.reshape(t, BW), s_a.reshape(t, BW), o_c.reshape(t, BW), proj,
                    w_glu_b[lyr], w_br_b[lyr], w_out_b[lyr], tl["merge_tm"])
        x2 = _ffn(x2, norm_ffn_g[lyr], w_up_b[lyr], conv_w[lyr], w_down_b[lyr], tl["ffn_tm"], tl["ffn_tf"], l)
    return x2.reshape(nb, l, d)
```

```python
import functools
import math

import numpy as np
import jax
import jax.numpy as jnp
from jax import lax
from jax.experimental import pallas as pl
from jax.experimental.pallas import tpu as pltpu

F32 = jnp.float32
BF16 = jnp.bfloat16
EPS = 1e-6
LOG2_E = math.log2(math.e)

N_HEADS = 4
HEAD_DIM = 128
BW = N_HEADS * HEAD_DIM
S5_GROUP = 16
S5_STATE = 64
S5_GROUPS = BW // S5_GROUP
CONV_W = 3

LANES = 128
SUBLANES = 8
VMEM_LIMIT_BYTES = 56 * 1024 * 1024

HG_BLOCK = 128
HG_LEVELS = (64, 32, 16, 8, 4, 2, 1)
S5_TILE = 8
S5_QGROUPS = LANES // S5_GROUP
S5_NQ = BW // LANES
S5_QSTATE = S5_QGROUPS * S5_STATE
SB_TK = 256


def _dot(a, b):
    return jnp.dot(a, b, preferred_element_type=F32)


def _dot_nt(a, b):
    return lax.dot_general(a, b, (((1,), (1,)), ((), ())), preferred_element_type=F32)


def _dot_tn(a, b):
    return lax.dot_general(a, b, (((0,), (0,)), ((), ())), preferred_element_type=F32)


def _split2(x):
    hi = x.astype(BF16)
    lo = (x - hi.astype(F32)).astype(BF16)
    return hi, lo


def _neg_abs(x):
    bits = lax.bitcast_convert_type(x, jnp.uint32) | jnp.uint32(0x80000000)
    return lax.bitcast_convert_type(bits, F32)


def _sigmoid(x):
    return 1.0 / (1.0 + jnp.exp(-x))


def _silu(x):
    return x * _sigmoid(x)


def _gelu_tanh(x):
    c = math.sqrt(2.0 / math.pi)
    return 0.5 * x * (1.0 + jnp.tanh(c * (x + 0.044715 * (x * x * x))))


def _rms(x, g):
    ms = jnp.mean(x * x, axis=-1, keepdims=True)
    return x * lax.rsqrt(ms + EPS) * g


def _cparams(sem):
    return pltpu.CompilerParams(dimension_semantics=sem, vmem_limit_bytes=VMEM_LIMIT_BYTES)


def _inproj_kernel(x_ref, g_ref, w_ref, o_ref, h_ref):
    @pl.when(pl.program_id(1) == 0)
    def _():
        h_ref[...] = _rms(x_ref[...], g_ref[...]).astype(BF16)

    o_ref[...] = _dot(h_ref[...], w_ref[...]).astype(o_ref.dtype)


def _inproj(x2, g, w, tm, tn):
    t, d = x2.shape
    n = w.shape[1]
    return pl.pallas_call(
        _inproj_kernel,
        grid=(t // tm, n // tn),
        in_specs=[
            pl.BlockSpec((tm, d), lambda i, j: (i, 0)),
            pl.BlockSpec((1, d), lambda i, j: (0, 0)),
            pl.BlockSpec((d, tn), lambda i, j: (0, j)),
        ],
        out_specs=pl.BlockSpec((tm, tn), lambda i, j: (i, j)),
        out_shape=jax.ShapeDtypeStruct((t, n), BF16),
        scratch_shapes=[pltpu.VMEM((tm, d), BF16)],
        compiler_params=_cparams(("arbitrary", "arbitrary")),
        name="inproj",
    )(x2, g.reshape(1, d), w)


def _hgrn2_decay_matrix():
    c = HG_BLOCK
    r = np.arange(c)[:, None]
    j = np.arange(c)[None, :]
    blocks = [(j <= r), (j > r)]
    for lvl in HG_LEVELS:
        ref = (r // (2 * lvl)) * (2 * lvl) + lvl - 1
        upper = (r % (2 * lvl)) >= lvl
        blocks.append(np.where(upper, (j > ref) & (j <= r), (j > r) & (j <= ref)))
    return np.concatenate(blocks, axis=0).astype(np.float32)


def _hgrn2_kernel(q_ref, f_ref, i_ref, g_ref, p_ref, a_ref, o_ref, st_ref):
    c = HG_BLOCK
    nb = q_ref.shape[0]

    @pl.when(pl.program_id(0) == 0)
    def _():
        st_ref[...] = jnp.zeros_like(st_ref)

    la = p_ref[0:1, :]
    l1 = p_ref[1:2, :]
    one_m_lb = p_ref[2:3, :]
    norm_g = p_ref[3:4, :]
    amat = a_ref[...]

    row = lax.broadcasted_iota(jnp.int32, (c, c), 0)
    col = lax.broadcasted_iota(jnp.int32, (c, c), 1)

    for b in range(nb):
        z = f_ref[b].astype(F32)
        sp = jnp.log(1.0 + jnp.exp(-jnp.abs(z)))
        y = l1 + (jnp.minimum(z, 0.0) - sp)
        logf = jnp.maximum(la, y) + jnp.log(1.0 + jnp.exp(-jnp.abs(la - y)))
        kk = one_m_lb * jnp.exp(jnp.minimum(-z, 0.0) - sp)
        lf_hi, lf_lo = _split2(logf)
        dec = jnp.exp(_dot(amat, lf_hi) + _dot(amat, lf_lo))
        q = q_ref[b].astype(F32)
        v = i_ref[b]
        gate = g_ref[b].astype(F32)
        for h in range(N_HEADS):
            sl = slice(h * HEAD_DIM, (h + 1) * HEAD_DIM)
            qh, kh, vh = q[:, sl], kk[:, sl], v[:, sl]
            scores = jnp.where(row == col, _dot_nt(qh.astype(BF16), kh.astype(BF16)), 0.0)
            for li, lvl in enumerate(HG_LEVELS):
                e = dec[(2 + li) * c:(3 + li) * c, sl]
                sh = int(math.log2(lvl))
                upper = ((row >> sh) & 1) == 1
                xq = jnp.where(upper, e * qh, 0.0).astype(BF16)
                xk = jnp.where(upper, 0.0, e * kh).astype(BF16)
                same = (row >> (sh + 1)) == (col >> (sh + 1))
                scores = scores + jnp.where(same, _dot_nt(xq, xk), 0.0)
            st = st_ref[b, h]
            qdec = (qh * dec[0:c, sl]).astype(BF16)
            o = _dot(scores.astype(BF16), vh) + _dot_nt(qdec, st.astype(BF16))
            kdec = (kh * dec[c:2 * c, sl]).astype(BF16)
            st_ref[b, h] = st * dec[c - 1:c, sl] + _dot_tn(vh, kdec)
            o = o * lax.rsqrt(jnp.mean(o * o, axis=-1, keepdims=True) + EPS)
            o_ref[b, :, sl] = (o * norm_g[:, sl] * _silu(gate[:, sl])).astype(o_ref.dtype)


def _hgrn2(proj3, params, amat):
    nb, l, _ = proj3.shape
    c = HG_BLOCK

    def col(k):
        return pl.BlockSpec((nb, c, BW), lambda i, k=k: (0, i, k))

    return pl.pallas_call(
        _hgrn2_kernel,
        grid=(l // c,),
        in_specs=[col(0), col(1), col(2), col(3),
                  pl.BlockSpec(params.shape, lambda i: (0, 0)),
                  pl.BlockSpec(amat.shape, lambda i: (0, 0))],
        out_specs=pl.BlockSpec((nb, c, BW), lambda i: (0, i, 0)),
        out_shape=jax.ShapeDtypeStruct((nb, l, BW), BF16),
        scratch_shapes=[pltpu.VMEM((nb, N_HEADS, HEAD_DIM, HEAD_DIM), F32)],
        compiler_params=_cparams(("arbitrary",)),
        name="hgrn2",
    )(proj3, proj3, proj3, proj3, params, amat)


def _s5_tables(a_re, a_im, b_re, b_im, c_re, c_im, d_skip, log_dt):
    r = S5_TILE
    dt = jnp.exp(log_dt)[:, None]
    lr, li = a_re, a_im
    mag = jnp.exp(lr * dt)
    ab_re, ab_im = mag * jnp.cos(li * dt), mag * jnp.sin(li * dt)
    den = lr * lr + li * li
    nr, ni = ab_re - 1.0, ab_im
    z_re = (nr * lr + ni * li) / den
    z_im = (ni * lr - nr * li) / den
    bb_re = z_re[..., None] * b_re - z_im[..., None] * b_im
    bb_im = z_re[..., None] * b_im + z_im[..., None] * b_re
    def powers(k):
        k = k[:, None, None]
        pmag = jnp.exp(k * (lr * dt))
        return pmag * jnp.cos(k * (li * dt)), pmag * jnp.sin(k * (li * dt))

    p_re, p_im = powers(jnp.arange(r + 1, dtype=F32))

    nq, qg, n, gc = S5_NQ, S5_QGROUPS, S5_STATE, S5_GROUP
    eye = jnp.eye(qg, dtype=F32)

    pw_re, pw_im = powers(r - 1.0 - jnp.arange(r, dtype=F32))
    wb_re = pw_re[..., None] * bb_re[None] - pw_im[..., None] * bb_im[None]
    wb_im = pw_re[..., None] * bb_im[None] + pw_im[..., None] * bb_re[None]

    def expand_b(w):
        w = w.reshape(r, nq, qg, n, gc)
        w = jnp.einsum('jqgnc,gh->qjgchn', w, eye)
        return w.reshape(nq, r * qg * gc, qg * n)

    wb = jnp.concatenate([expand_b(wb_re), expand_b(wb_im)], axis=-1)

    q_re, q_im = p_re[1:], p_im[1:]
    cc_re = c_re[None] * q_re[:, :, None, :] - c_im[None] * q_im[:, :, None, :]
    cc_im = c_re[None] * q_im[:, :, None, :] + c_im[None] * q_re[:, :, None, :]

    def expand_c(w):
        w = w.reshape(r, nq, qg, gc, n)
        w = jnp.einsum('iqgcn,gh->qgnihc', w, eye)
        return w.reshape(nq, qg * n, r * qg * gc)

    wc = jnp.concatenate([expand_c(cc_re), -expand_c(cc_im)], axis=1)

    cb_re = (jnp.einsum('gcn,lgn,gnd->lgcd', c_re, p_re[:r], bb_re) - jnp.einsum('gcn,lgn,gnd->lgcd', c_re, p_im[:r], bb_im)
             - jnp.einsum('gcn,lgn,gnd->lgcd', c_im, p_re[:r], bb_im) - jnp.einsum('gcn,lgn,gnd->lgcd', c_im, p_im[:r], bb_re))
    lag = jnp.arange(r)[None, :] - jnp.arange(r)[:, None]
    kt = jnp.where((lag >= 0)[:, :, None, None, None], cb_re[jnp.clip(lag, 0, r - 1)], 0.0)
    kt = kt.reshape(r, r, nq, qg, gc, gc)
    wk = jnp.einsum('jiqgcd,gh->qjgdihc', kt, eye).reshape(nq, r * qg * gc, r * qg * gc)

    a8 = jnp.concatenate([p_re[r].reshape(1, -1), p_im[r].reshape(1, -1)], axis=0)
    d8 = jnp.tile(d_skip.reshape(1, BW), (1, r))
    return wb.astype(BF16), wk.astype(BF16), wc.astype(BF16), a8, d8


def _s5_kernel(u_ref, wb_ref, wk_ref, wc_ref, a8_ref, d8_ref, o_ref, z_ref, xp_ref, x_ref):
    rows = u_ref.shape[1]
    ns = S5_GROUPS * S5_STATE
    qs = S5_QSTATE

    @pl.when(pl.program_id(1) == 0)
    def _():
        x_ref[...] = jnp.zeros_like(x_ref)

    def gather_q(ref, q):
        return jnp.concatenate(
            [ref[0, :, j * BW + q * LANES:j * BW + (q + 1) * LANES] for j in range(S5_TILE)], axis=-1)

    for q in range(S5_NQ):
        zq = _dot(gather_q(u_ref, q), wb_ref[q])
        z_ref[:, q * qs:(q + 1) * qs] = zq[:, :qs]
        z_ref[:, ns + q * qs:ns + (q + 1) * qs] = zq[:, qs:]

    ar, ai = a8_ref[0:1, :], a8_ref[1:2, :]

    def step(r, carry):
        xr, xi = carry
        xp_ref[pl.ds(r, 1), 0:ns] = xr
        xp_ref[pl.ds(r, 1), ns:2 * ns] = xi
        zr = z_ref[pl.ds(r, 1), 0:ns]
        zi = z_ref[pl.ds(r, 1), ns:2 * ns]
        return ar * xr - ai * xi + zr, ar * xi + ai * xr + zi

    xr, xi = lax.fori_loop(0, rows, step, (x_ref[0:1, 0:ns], x_ref[0:1, ns:2 * ns]))
    x_ref[0:1, 0:ns] = xr
    x_ref[0:1, ns:2 * ns] = xi

    for q in range(S5_NQ):
        xq = jnp.concatenate([xp_ref[:, q * qs:(q + 1) * qs],
                              xp_ref[:, ns + q * qs:ns + (q + 1) * qs]], axis=-1).astype(BF16)
        yq = _dot(gather_q(u_ref, q), wk_ref[q]) + _dot(xq, wc_ref[q])
        for i in range(S5_TILE):
            sl = slice(i * BW + q * LANES, i * BW + (q + 1) * LANES)
            y = yq[:, i * LANES:(i + 1) * LANES] + d8_ref[:, sl] * u_ref[0, :, sl].astype(F32)
            o_ref[0, :, sl] = _gelu_tanh(y).astype(o_ref.dtype)


def _s5(u8, tables, rows):
    nb, lr, w = u8.shape
    wb, wk, wc, a8, d8 = tables
    full3 = lambda b, i: (0, 0, 0)
    full2 = lambda b, i: (0, 0)
    return pl.pallas_call(
        _s5_kernel,
        grid=(nb, lr // rows),
        in_specs=[pl.BlockSpec((1, rows, w), lambda b, i: (b, i, 0)),
                  pl.BlockSpec(wb.shape, full3), pl.BlockSpec(wk.shape, full3), pl.BlockSpec(wc.shape, full3),
                  pl.BlockSpec(a8.shape, full2), pl.BlockSpec(d8.shape, full2)],
        out_specs=pl.BlockSpec((1, rows, w), lambda b, i: (b, i, 0)),
        out_shape=jax.ShapeDtypeStruct((nb, lr, w), BF16),
        scratch_shapes=[pltpu.VMEM((rows, w), F32), pltpu.VMEM((rows, w), F32), pltpu.VMEM((SUBLANES, w), F32)],
        compiler_params=_cparams(("arbitrary", "arbitrary")),
        name="s5",
    )(u8, wb, wk, wc, a8, d8)


def _stickbreak_kernel(q_ref, k_ref, v_ref, qg_ref, kg_ref, u_ref, o_ref,
                       kh_ref, acc_ref, later_ref, z_ref, n_ref, t_ref):
    tq = q_ref.shape[1]
    l = k_ref.shape[1]
    tk = SB_TK
    pair = 2 * tk
    assert tq == pair and l % min(l, 1024) == 0
    qi = pl.program_id(2)
    kchunk = min(l, 1024)

    @pl.when(qi == 0)
    def _():
        def body(c, _):
            r0 = pl.multiple_of(c * kchunk, kchunk)
            kh_ref[pl.ds(r0, kchunk), :] = _rms(k_ref[0, pl.ds(r0, kchunk), :].astype(F32), kg_ref[...]).astype(BF16)
            return 0
        lax.fori_loop(0, l // kchunk, body, 0)

    qh = (_rms(q_ref[0].astype(F32), qg_ref[...]) * (HEAD_DIM ** -0.5 * LOG2_E)).astype(BF16)
    umat = u_ref[...]
    acc_ref[...] = jnp.zeros_like(acc_ref)
    later_ref[...] = jnp.zeros_like(later_ref)

    def key0(p):
        return pl.multiple_of((qi - p) * pair, pair)

    def qk(p):
        return _dot_nt(qh, kh_ref[pl.ds(key0(p), pair), :])

    def logits(p, z, masked):
        nlk = jnp.maximum(z, 0.0) + jnp.log2(1.0 + jnp.exp2(_neg_abs(z)))
        if masked:
            qpos = qi * tq + lax.broadcasted_iota(jnp.int32, (tq, pair), 0)
            kpos = key0(p) + lax.broadcasted_iota(jnp.int32, (tq, pair), 1)
            keep = kpos < qpos
            nlk = jnp.where(keep, nlk, 0.0)
            z = jnp.where(keep, z, -jnp.inf)
        z_ref[...] = z
        n_ref[...] = nlk.astype(BF16)

    def exponents():
        later = later_ref[...]
        for half in (1, 0):
            sl = slice(half * tk, (half + 1) * tk)
            incl = _dot(n_ref[:, sl], umat)
            t_ref[:, sl] = z_ref[:, sl] + incl + jnp.concatenate([later] * (tk // LANES), axis=1)
            later = later + jnp.broadcast_to(incl[:, 0:1], later.shape)
        later_ref[...] = later

    def accumulate(p):
        w = jnp.exp2(t_ref[...]).astype(BF16)
        acc_ref[...] += _dot(w, v_ref[0, pl.ds(key0(p), pair), :])

    logits(0, qk(0), True)

    @pl.when(qi >= 1)
    def _():
        exponents()
        logits(1, qk(1), False)

        def body(n, _):
            accumulate(n - 2)
            exponents()
            logits(n, qk(n), False)
            return 0

        lax.fori_loop(2, qi + 1, body, 0)
        accumulate(qi - 1)

    exponents()
    accumulate(qi)
    o_ref[0] = acc_ref[...].astype(o_ref.dtype)


def _stickbreak(proj3, q_g, k_g, tq):
    nb, l, _ = proj3.shape
    tk = SB_TK
    umat = jnp.asarray(-np.tril(np.ones((tk, tk), np.float32)), BF16)
    qcol, kcol, vcol = 5 * N_HEADS, 6 * N_HEADS, 7 * N_HEADS
    return pl.pallas_call(
        _stickbreak_kernel,
        grid=(nb, N_HEADS, l // tq),
        in_specs=[pl.BlockSpec((1, tq, HEAD_DIM), lambda b, h, i: (b, i, qcol + h)),
                  pl.BlockSpec((1, l, HEAD_DIM), lambda b, h, i: (b, 0, kcol + h)),
                  pl.BlockSpec((1, l, HEAD_DIM), lambda b, h, i: (b, 0, vcol + h)),
                  pl.BlockSpec((1, HEAD_DIM), lambda b, h, i: (0, 0)),
                  pl.BlockSpec((1, HEAD_DIM), lambda b, h, i: (0, 0)),
                  pl.BlockSpec((tk, tk), lambda b, h, i: (0, 0))],
        out_specs=pl.BlockSpec((1, tq, HEAD_DIM), lambda b, h, i: (b, i, h)),
        out_shape=jax.ShapeDtypeStruct((nb, l, BW), BF16),
        scratch_shapes=[pltpu.VMEM((l, HEAD_DIM), BF16), pltpu.VMEM((tq, HEAD_DIM), F32),
                        pltpu.VMEM((tq, LANES), F32), pltpu.VMEM((tq, 2 * tk), F32),
                        pltpu.VMEM((tq, 2 * tk), BF16), pltpu.VMEM((tq, 2 * tk), F32)],
        compiler_params=_cparams(("arbitrary", "arbitrary", "arbitrary")),
        name="stickbreak",
    )(proj3, proj3, proj3, q_g.reshape(1, HEAD_DIM), k_g.reshape(1, HEAD_DIM), umat)


def _merge_kernel(x_ref, oa_ref, sa_ref, oc_ref, ga_ref, gb_ref, gc_ref, wglu_ref, wbr_ref, wout_ref, o_ref):
    hg = _dot(sa_ref[...], wglu_ref[...])
    ob = (hg[:, :BW] * _sigmoid(hg[:, BW:])).astype(BF16)
    m = (_sigmoid(ga_ref[...].astype(F32)) * _dot(oa_ref[...], wbr_ref[0])
         + _sigmoid(gb_ref[...].astype(F32)) * _dot(ob, wbr_ref[1])
         + _sigmoid(gc_ref[...].astype(F32)) * _dot(oc_ref[...], wbr_ref[2]))
    o_ref[...] = x_ref[...] + _dot(m.astype(BF16), wout_ref[...])


def _merge(x2, oa, sa, oc, proj, wglu, wbr, wout, tm):
    t, d = x2.shape
    gate0 = (8 * BW) // d
    row = lambda i: (i, 0)
    const = pl.Buffered(1)
    return pl.pallas_call(
        _merge_kernel,
        grid=(t // tm,),
        in_specs=[pl.BlockSpec((tm, d), row),
                  pl.BlockSpec((tm, BW), row), pl.BlockSpec((tm, BW), row), pl.BlockSpec((tm, BW), row),
                  pl.BlockSpec((tm, d), lambda i: (i, gate0)),
                  pl.BlockSpec((tm, d), lambda i: (i, gate0 + 1)),
                  pl.BlockSpec((tm, d), lambda i: (i, gate0 + 2)),
                  pl.BlockSpec(wglu.shape, lambda i: (0, 0), pipeline_mode=const),
                  pl.BlockSpec(wbr.shape, lambda i: (0, 0, 0), pipeline_mode=const),
                  pl.BlockSpec(wout.shape, lambda i: (0, 0), pipeline_mode=const)],
        out_specs=pl.BlockSpec((tm, d), row),
        out_shape=jax.ShapeDtypeStruct((t, d), F32),
        compiler_params=_cparams(("arbitrary",)),
        name="merge",
    )(x2, oa, sa, oc, proj, proj, proj, wglu, wbr, wout)


def _ffn_kernel(xh_ref, x_ref, g_ref, wa_ref, wb_ref, cwa_ref, cwb_ref, wd_ref, o_ref, hn_ref, *, tiles_per_seq):
    i = pl.program_id(0)
    halo = SUBLANES

    @pl.when(pl.program_id(1) == 0)
    def _():
        first = (i % tiles_per_seq) == 0
        hh = _rms(xh_ref[...], g_ref[...])
        hn_ref[0:halo, :] = jnp.where(first, 0.0, hh).astype(BF16)
        hn_ref[halo:, :] = _rms(x_ref[...], g_ref[...]).astype(BF16)
        o_ref[...] = x_ref[...]

    hn = hn_ref[...]

    def conv(u, cw):
        return cw[2:3, :] * u[halo:, :] + cw[1:2, :] * u[halo - 1:-1, :] + cw[0:1, :] * u[halo - 2:-2, :]

    ca = conv(_dot(hn, wa_ref[...]), cwa_ref[...])
    cb = conv(_dot(hn, wb_ref[...]), cwb_ref[...])
    act = (_silu(ca) * cb).astype(BF16)
    o_ref[...] += _dot(act, wd_ref[...])


def _ffn(x2, g, w_up, conv_w, w_down, tm, tf, seq_len):
    t, d = x2.shape
    dff = w_down.shape[0]
    nf = dff // tf
    hb = tm // SUBLANES
    return pl.pallas_call(
        functools.partial(_ffn_kernel, tiles_per_seq=seq_len // tm),
        grid=(t // tm, nf),
        in_specs=[pl.BlockSpec((SUBLANES, d), lambda i, j: (jnp.maximum(i * hb - 1, 0), 0)),
                  pl.BlockSpec((tm, d), lambda i, j: (i, 0)),
                  pl.BlockSpec((1, d), lambda i, j: (0, 0)),
                  pl.BlockSpec((d, tf), lambda i, j: (0, j)),
                  pl.BlockSpec((d, tf), lambda i, j: (0, j + nf)),
                  pl.BlockSpec((CONV_W, tf), lambda i, j: (0, j)),
                  pl.BlockSpec((CONV_W, tf), lambda i, j: (0, j + nf)),
                  pl.BlockSpec((tf, d), lambda i, j: (j, 0))],
        out_specs=pl.BlockSpec((tm, d), lambda i, j: (i, 0)),
        out_shape=jax.ShapeDtypeStruct((t, d), F32),
        scratch_shapes=[pltpu.VMEM((tm + SUBLANES, d), BF16)],
        compiler_params=_cparams(("arbitrary", "arbitrary")),
        name="ffn",
    )(x2, x2, g.reshape(1, d), w_up, w_up, conv_w, conv_w, w_down)


def _tiles(seq_len):
    return dict(
        inproj_tm=min(1024, seq_len), inproj_tn=1024,
        s5_rows=min(256, seq_len // S5_TILE),
        sb_tq=2 * SB_TK,
        merge_tm=min(256, seq_len),
        ffn_tm=min(512, seq_len), ffn_tf=512,
    )


def kernel(x, w_in, norm_mix_g, hg_lb_logits, hg_norm_g, s5_a_re, s5_a_im, s5_b_re, s5_b_im, s5_c_re, s5_c_im, s5_d, s5_log_dt, s5_w_glu, sb_q_g, sb_k_g, w_branch, w_out, norm_ffn_g, w_up, conv_w, w_down):
    nb, l, d = x.shape
    depth = w_in.shape[0]
    t = nb * l
    tl = _tiles(l)
    assert l % HG_BLOCK == 0 and l % (2 * SB_TK) == 0 and w_in.shape[2] == 8 * BW + 3 * d

    lb = jnp.cumsum(jax.nn.softmax(hg_lb_logits.astype(F32), axis=0), axis=0)
    lb = lb - lb[0:1]
    zeros = jnp.zeros_like(lb)
    hg_params = jnp.stack([jnp.log(lb), jnp.log1p(-lb), 1.0 - lb, hg_norm_g.astype(F32),
                           zeros, zeros, zeros, zeros], axis=1)
    amat = jnp.asarray(_hgrn2_decay_matrix(), BF16)

    w_in_b, w_glu_b, w_br_b = w_in.astype(BF16), s5_w_glu.astype(BF16), w_branch.astype(BF16)
    w_out_b, w_up_b, w_down_b = w_out.astype(BF16), w_up.astype(BF16), w_down.astype(BF16)

    x2 = x.reshape(t, d)
    for lyr in range(depth):
        proj = _inproj(x2, norm_mix_g[lyr], w_in_b[lyr], tl["inproj_tm"], tl["inproj_tn"])
        proj3 = proj.reshape(nb, l, -1)
        o_a = _hgrn2(proj3, hg_params[lyr], amat)
        tables = _s5_tables(s5_a_re[lyr], s5_a_im[lyr], s5_b_re[lyr], s5_b_im[lyr], s5_c_re[lyr], s5_c_im[lyr],
                            s5_d[lyr], s5_log_dt[lyr])
        u8 = proj3[:, :, 4 * BW:5 * BW].reshape(nb, l // S5_TILE, S5_TILE * BW)
        s_a = _s5(u8, tables, tl["s5_rows"])
        o_c = _stickbreak(proj3, sb_q_g[lyr], sb_k_g[lyr], tl["sb_tq"])
        x2 = _merge(x2, o_a.reshape(t, BW), s_a.reshape(t, BW), o_c.reshape(t, BW), proj,
                    w_glu_b[lyr], w_br_b[lyr], w_out_b[lyr], tl["merge_tm"])
        x2 = _ffn(x2, norm_ffn_g[lyr], w_up_b[lyr], conv_w[lyr], w_down_b[lyr], tl["ffn_tm"], tl["ffn_tf"], l)
    return x2.reshape(nb, l, d)
```
